```python
import math
import jax, jax.numpy as jnp
from jax import lax
import numpy as np

D_MODEL = 1024
BATCH = 4
SEQ = 8192
DEPTH = 2

HEAD_DIM = 64
DIFF_HEADS = D_MODEL // (4 * HEAD_DIM)
DIFF_WIDTH = DIFF_HEADS * 2 * HEAD_DIM
SB_WIDTH = D_MODEL - DIFF_WIDTH
SB_HEADS = SB_WIDTH // HEAD_DIM
D_FF = ((8 * D_MODEL // 3 + 255) // 256) * 256
CONV_WIDTH = 3
BLOCK_Q = 128
ROPE_THETA = 10000.0
LN_EPS = 1e-5
RMS_EPS = 1e-6
LAMBDA_STD = 0.1
DEEPNORM_ALPHA = (2 * DEPTH) ** 0.25
DEEPNORM_BETA = (8 * DEPTH) ** -0.25

kernel_name = "hymba_diff_stickbreak_deepnorm"


def layer_norm(x, g, b):
    x32 = x.astype(jnp.float32)
    mu = jnp.mean(x32, axis=-1, keepdims=True)
    xc = x32 - mu
    var = jnp.mean(xc * xc, axis=-1, keepdims=True)
    y = xc * lax.rsqrt(var + LN_EPS) * g.astype(jnp.float32) + b.astype(jnp.float32)
    return y.astype(x.dtype)


def rms_norm_f32(x32, g):
    return x32 * lax.rsqrt(jnp.mean(x32 * x32, axis=-1, keepdims=True) + RMS_EPS) * g.astype(jnp.float32)


def rope_tables(seq):
    inv = 1.0 / (ROPE_THETA ** (jnp.arange(0, HEAD_DIM, 2, dtype=jnp.float32) / HEAD_DIM))
    ang = jnp.arange(seq, dtype=jnp.float32)[:, None] * inv[None, :]
    return jnp.cos(ang), jnp.sin(ang)


def apply_rope(x, cos, sin):
    x32 = x.astype(jnp.float32)
    half = HEAD_DIM // 2
    x1, x2 = x32[..., :half], x32[..., half:]
    return jnp.concatenate([x1 * cos - x2 * sin, x2 * cos + x1 * sin], axis=-1).astype(x.dtype)


def diff_attention(q, k, v, lam, lam_init, norm_g):
    seq = q.shape[3]
    scale = HEAD_DIM ** -0.5
    outs = []
    for i in range(seq // BLOCK_Q):
        lo, hi = i * BLOCK_Q, (i + 1) * BLOCK_Q
        s = jnp.einsum('bhcqd,bhckd->bhcqk', q[:, :, :, lo:hi], k[:, :, :, :hi]).astype(jnp.float32) * scale
        causal = jnp.arange(hi)[None, :] <= jnp.arange(lo, hi)[:, None]
        p = jax.nn.softmax(jnp.where(causal, s, -jnp.inf), axis=-1)
        w = p[:, :, 0] - lam * p[:, :, 1]
        outs.append(jnp.einsum('bhqk,bhke->bhqe', w.astype(v.dtype), v[:, :, :hi]))
    o = jnp.concatenate(outs, axis=2).astype(jnp.float32)
    o = rms_norm_f32(o, norm_g) * (1.0 - lam_init)
    return o.astype(v.dtype)


def stick_breaking_attention(q, k, v, norm_g):
    seq = q.shape[2]
    scale = HEAD_DIM ** -0.5
    outs = []
    for i in range(seq // BLOCK_Q):
        lo, hi = i * BLOCK_Q, (i + 1) * BLOCK_Q
        z = jnp.einsum('bhqd,bhkd->bhqk', q[:, :, lo:hi], k[:, :, :hi]).astype(jnp.float32) * scale
        strict = jnp.arange(hi)[None, :] < jnp.arange(lo, hi)[:, None]
        log_1m_beta = jnp.where(strict, jax.nn.log_sigmoid(-z), 0.0)
        later = lax.cumsum(log_1m_beta, axis=log_1m_beta.ndim - 1, reverse=True) - log_1m_beta
        a = jnp.where(strict, jnp.exp(jax.nn.log_sigmoid(z) + later), 0.0)
        outs.append(jnp.einsum('bhqk,bhkd->bhqd', a.astype(v.dtype), v[:, :, :hi]))
    o = jnp.concatenate(outs, axis=2).astype(jnp.float32)
    return rms_norm_f32(o, norm_g).astype(v.dtype)


def causal_depthwise_conv(h, w, b):
    seq = h.shape[1]
    hp = jnp.pad(h, ((0, 0), (CONV_WIDTH - 1, 0), (0, 0)))
    out = b + hp[:, 0:seq] * w[0]
    for j in range(1, CONV_WIDTH):
        out = out + hp[:, j:j + seq] * w[j]
    return out


def setup_inputs(seed: int = 0) -> dict:
    key = jax.random.key(seed)
    ks = jax.random.split(key, 20)
    f32 = jnp.float32
    d, f = D_MODEL, D_FF
    nrm = lambda k, shape, s: jax.random.normal(k, shape, f32) * s
    return {
        "x": nrm(ks[0], (BATCH, SEQ, d), 1.0),
        "w_in": nrm(ks[1], (DEPTH, d, 3 * d), d ** -0.5),
        "w_out": nrm(ks[2], (DEPTH, d, d), d ** -0.5 * DEEPNORM_BETA),
        "lam_q1": nrm(ks[3], (DEPTH, HEAD_DIM), LAMBDA_STD),
        "lam_k1": nrm(ks[4], (DEPTH, HEAD_DIM), LAMBDA_STD),
        "lam_q2": nrm(ks[5], (DEPTH, HEAD_DIM), LAMBDA_STD),
        "lam_k2": nrm(ks[6], (DEPTH, HEAD_DIM), LAMBDA_STD),
        "diff_norm_g": 1.0 + nrm(ks[7], (DEPTH, 2 * HEAD_DIM), 0.02),
        "sb_norm_g": 1.0 + nrm(ks[8], (DEPTH, HEAD_DIM), 0.02),
        "ln1_g": 1.0 + nrm(ks[9], (DEPTH, d), 0.02),
        "ln1_b": nrm(ks[10], (DEPTH, d), 0.02),
        "w_up": nrm(ks[11], (DEPTH, d, 2 * f), d ** -0.5 * DEEPNORM_BETA),
        "conv_w": nrm(ks[12], (DEPTH, CONV_WIDTH, 2 * f), CONV_WIDTH ** -0.5),
        "conv_b": nrm(ks[13], (DEPTH, 2 * f), 0.01),
        "w_down": nrm(ks[14], (DEPTH, f, d), f ** -0.5 * DEEPNORM_BETA),
        "ln2_g": 1.0 + nrm(ks[15], (DEPTH, d), 0.02),
        "ln2_b": nrm(ks[16], (DEPTH, d), 0.02),
    }


def reference(x, w_in, w_out, lam_q1, lam_k1, lam_q2, lam_k2, diff_norm_g, sb_norm_g,
              ln1_g, ln1_b, w_up, conv_w, conv_b, w_down, ln2_g, ln2_b):
    bsz, seq, _ = x.shape
    cos, sin = rope_tables(seq)
    o1 = DIFF_WIDTH
    o2 = 2 * DIFF_WIDTH
    o3 = 3 * DIFF_WIDTH
    o4 = o3 + SB_WIDTH
    o5 = o4 + SB_WIDTH
    for l in range(DEPTH):
        lam_init = 0.8 - 0.6 * math.exp(-0.3 * l)
        lam = (jnp.exp(jnp.sum(lam_q1[l].astype(jnp.float32) * lam_k1[l].astype(jnp.float32)))
               - jnp.exp(jnp.sum(lam_q2[l].astype(jnp.float32) * lam_k2[l].astype(jnp.float32)))
               + lam_init)
        h = x @ w_in[l]
        dq, dk, dv = h[..., :o1], h[..., o1:o2], h[..., o2:o3]
        sq, sk, sv = h[..., o3:o4], h[..., o4:o5], h[..., o5:]
        dq = apply_rope(dq.reshape(bsz, seq, DIFF_HEADS, 2, HEAD_DIM).transpose(0, 2, 3, 1, 4), cos, sin)
        dk = apply_rope(dk.reshape(bsz, seq, DIFF_HEADS, 2, HEAD_DIM).transpose(0, 2, 3, 1, 4), cos, sin)
        dv = dv.reshape(bsz, seq, DIFF_HEADS, 2 * HEAD_DIM).transpose(0, 2, 1, 3)
        sq = sq.reshape(bsz, seq, SB_HEADS, HEAD_DIM).transpose(0, 2, 1, 3)
        sk = sk.reshape(bsz, seq, SB_HEADS, HEAD_DIM).transpose(0, 2, 1, 3)
        sv = sv.reshape(bsz, seq, SB_HEADS, HEAD_DIM).transpose(0, 2, 1, 3)
        d_out = diff_attention(dq, dk, dv, lam, lam_init, diff_norm_g[l])
        s_out = stick_breaking_attention(sq, sk, sv, sb_norm_g[l])
        mixed = jnp.concatenate([
            d_out.transpose(0, 2, 1, 3).reshape(bsz, seq, DIFF_WIDTH),
            s_out.transpose(0, 2, 1, 3).reshape(bsz, seq, SB_WIDTH)], axis=-1)
        x = layer_norm(DEEPNORM_ALPHA * x + mixed @ w_out[l], ln1_g[l], ln1_b[l])
        u = causal_depthwise_conv(x @ w_up[l], conv_w[l], conv_b[l])
        g = jax.nn.silu(u[..., :D_FF]) * u[..., D_FF:]
        x = layer_norm(DEEPNORM_ALPHA * x + g @ w_down[l], ln2_g[l], ln2_b[l])
    return x
```

```python
import functools
import math

import jax
import jax.numpy as jnp
from jax import lax
from jax.experimental import pallas as pl
from jax.experimental.pallas import tpu as pltpu

D_MODEL = 1024
HEAD_DIM = 64
DIFF_HEADS = 4
DIFF_WIDTH = 512
SB_WIDTH = 512
SB_PAIRS = 4
D_FF = 2816
CONV_WIDTH = 3
ROPE_THETA = 10000.0
LN_EPS = 1e-5
RMS_EPS = 1e-6
DEPTH = 2
DEEPNORM_ALPHA = (2 * DEPTH) ** 0.25
QK_SCALE = HEAD_DIM ** -0.5

LANES = 128
SUBLANES = 8
VMEM_LIMIT = 56 * 1024 * 1024

PROJ_TM = 512
PROJ_GROUP = 512
ATT_TQ = 256
DIFF_TK = 512
SB_TK = 256
MLP_TM = 512
MLP_TF = 256
MLP_CHUNKS = D_FF // MLP_TF

F32 = jnp.float32
BF16 = jnp.bfloat16
_NT = (((1,), (1,)), ((), ()))


def _layer_norm(y, g, b):
    mu = jnp.mean(y, axis=-1, keepdims=True)
    yc = y - mu
    var = jnp.mean(yc * yc, axis=-1, keepdims=True)
    return yc * lax.rsqrt(var + LN_EPS) * g + b


def _in_proj_kernel(x_ref, w_ref, cos_ref, sin_ref, qd_ref, kd_ref, vd_ref, qs_ref, ks_ref, vs_ref):
    xb = x_ref[0].astype(BF16)
    cos = cos_ref[...]
    sin = sin_ref[...]
    lane = lax.broadcasted_iota(jnp.int32, cos.shape, 1)
    first_half = (lane & (HEAD_DIM - 1)) < (HEAD_DIM // 2)
    outs = (qd_ref, kd_ref, vd_ref, qs_ref, ks_ref, vs_ref)
    for grp, o_ref in enumerate(outs):
        acc = jnp.dot(xb, w_ref[:, grp * PROJ_GROUP:(grp + 1) * PROJ_GROUP], preferred_element_type=F32)
        for hh in range(PROJ_GROUP // LANES):
            a = acc[:, hh * LANES:(hh + 1) * LANES]
            if grp < 2:
                partner = jnp.where(first_half,
                                    pltpu.roll(a, LANES - HEAD_DIM // 2, 1),
                                    pltpu.roll(a, HEAD_DIM // 2, 1))
                a = a * cos + partner * sin
            if grp in (0, 3):
                a = a * QK_SCALE
            o_ref[0, hh] = a.astype(BF16)


def _in_proj(x, w_bf, cos_t, sin_t):
    bsz, seq, d = x.shape
    tm = PROJ_TM
    n_s = seq // tm
    head_major = jax.ShapeDtypeStruct((bsz, 4, seq, LANES), BF16)
    out_spec = pl.BlockSpec((1, 4, tm, LANES), lambda b, i: (b, 0, i, 0))
    return pl.pallas_call(
        _in_proj_kernel,
        grid=(bsz, n_s),
        in_specs=[
            pl.BlockSpec((1, tm, d), lambda b, i: (b, i, 0)),
            pl.BlockSpec((d, 3 * d), lambda b, i: (0, 0)),
            pl.BlockSpec((tm, LANES), lambda b, i: (i, 0)),
            pl.BlockSpec((tm, LANES), lambda b, i: (i, 0)),
        ],
        out_specs=[out_spec] * 6,
        out_shape=[head_major] * 6,
        compiler_params=pltpu.CompilerParams(
            dimension_semantics=("arbitrary", "arbitrary"), vmem_limit_bytes=VMEM_LIMIT),
        name="in_proj",
    )(x, w_bf, cos_t, sin_t)


def _stack_halves(q, qs_ref, tq):
    lane = lax.broadcasted_iota(jnp.int32, q.shape, 1)
    zero = jnp.zeros_like(q)
    qs_ref[0:tq, :] = jnp.where(lane < HEAD_DIM, q, zero)
    qs_ref[tq:2 * tq, :] = jnp.where(lane < HEAD_DIM, zero, q)


def _tile_counts(i, tq, tk):
    n_full = (i * tq) // tk
    n_tot = (i * tq + tq + tk - 1) // tk
    return n_full, n_tot


def _diff_attn_kernel(lq1_ref, lk1_ref, lq2_ref, lk2_ref, g_ref, q_ref, k_ref, v_ref, o_ref,
                      qs_ref, m_ref, l_ref, acc_ref, *, tq, tk, lam_init):
    i = pl.program_id(2)
    _stack_halves(q_ref[0, 0], qs_ref, tq)
    m_ref[...] = jnp.full(m_ref.shape, -jnp.inf, F32)
    l_ref[...] = jnp.zeros(l_ref.shape, F32)
    acc_ref[...] = jnp.zeros(acc_ref.shape, F32)
    n_full, n_tot = _tile_counts(i, tq, tk)

    def step(kj, masked):
        start = pl.multiple_of(kj * tk, tk)
        kt = k_ref[0, 0, pl.ds(start, tk), :]
        vt = v_ref[0, 0, pl.ds(start, tk), :]
        s = lax.dot_general(qs_ref[...], kt, _NT, preferred_element_type=F32)
        if masked:
            row = i * tq + (lax.broadcasted_iota(jnp.int32, s.shape, 0) & (tq - 1))
            col = start + lax.broadcasted_iota(jnp.int32, s.shape, 1)
            s = jnp.where(col <= row, s, -jnp.inf)
        m_old = m_ref[...]
        m_new = jnp.maximum(m_old, jnp.max(s, axis=1, keepdims=True))
        p = jnp.exp(s - m_new)
        alpha = jnp.exp(m_old - m_new)
        l_ref[...] = alpha * l_ref[...] + jnp.sum(p, axis=1, keepdims=True)
        acc_ref[...] = alpha * acc_ref[...] + jnp.dot(p.astype(BF16), vt, preferred_element_type=F32)
        m_ref[...] = m_new

    def full_body(kj, carry):
        step(kj, False)
        return carry

    def masked_body(kj, carry):
        step(kj, True)
        return carry

    lax.fori_loop(0, n_full, full_body, 0)
    lax.fori_loop(n_full, n_tot, masked_body, 0)

    lam = (jnp.exp(jnp.sum(lq1_ref[...] * lk1_ref[...], axis=1, keepdims=True))
           - jnp.exp(jnp.sum(lq2_ref[...] * lk2_ref[...], axis=1, keepdims=True)) + lam_init)
    o1 = acc_ref[0:tq, :] / l_ref[0:tq, :]
    o2 = acc_ref[tq:2 * tq, :] / l_ref[tq:2 * tq, :]
    o = o1 - lam * o2
    o = o * lax.rsqrt(jnp.mean(o * o, axis=1, keepdims=True) + RMS_EPS) * g_ref[...]
    o_ref[0] = (o * (1.0 - lam_init)).astype(BF16)


def _diff_attn(qd, kd, vd, lq1, lk1, lq2, lk2, g, lam_init):
    bsz, nh, seq, _ = qd.shape
    tq, tk = ATT_TQ, DIFF_TK
    vec64 = pl.BlockSpec((1, HEAD_DIM), lambda b, h, i: (0, 0))
    kv_spec = pl.BlockSpec((1, 1, seq, LANES), lambda b, h, i: (b, h, 0, 0))
    return pl.pallas_call(
        functools.partial(_diff_attn_kernel, tq=tq, tk=tk, lam_init=lam_init),
        grid=(bsz, nh, seq // tq),
        in_specs=[vec64, vec64, vec64, vec64,
                  pl.BlockSpec((1, LANES), lambda b, h, i: (0, 0)),
                  pl.BlockSpec((1, 1, tq, LANES), lambda b, h, i: (b, h, i, 0)),
                  kv_spec, kv_spec],
        out_specs=pl.BlockSpec((1, tq, LANES), lambda b, h, i: (b, i, h)),
        out_shape=jax.ShapeDtypeStruct((bsz, seq, nh * LANES), BF16),
        scratch_shapes=[pltpu.VMEM((2 * tq, LANES), BF16),
                        pltpu.VMEM((2 * tq, 1), F32),
                        pltpu.VMEM((2 * tq, 1), F32),
                        pltpu.VMEM((2 * tq, LANES), F32)],
        compiler_params=pltpu.CompilerParams(
            dimension_semantics=("arbitrary", "arbitrary", "arbitrary"), vmem_limit_bytes=VMEM_LIMIT),
        name="diff_attn",
    )(lq1, lk1, lq2, lk2, g, qd, kd, vd)


def _sb_attn_kernel(g_ref, q_ref, k_ref, v_ref, o_ref, qs_ref, c_ref, acc_ref, *, tq, tk):
    i = pl.program_id(2)
    _stack_halves(q_ref[0, 0], qs_ref, tq)
    c_ref[...] = jnp.zeros(c_ref.shape, F32)
    acc_ref[...] = jnp.zeros(acc_ref.shape, F32)
    n_full, n_tot = _tile_counts(i, tq, tk)
    suffix = (lax.broadcasted_iota(jnp.int32, (tk, tk), 0)
              > lax.broadcasted_iota(jnp.int32, (tk, tk), 1)).astype(BF16)

    def step(kj, masked):
        start = pl.multiple_of(kj * tk, tk)
        kt = k_ref[0, 0, pl.ds(start, tk), :]
        vt = v_ref[0, 0, pl.ds(start, tk), :]
        z = lax.dot_general(qs_ref[...], kt, _NT, preferred_element_type=F32)
        softplus = jnp.maximum(z, 0.0) + jnp.log(1.0 + jnp.exp(-jnp.abs(z)))
        log_keep = -softplus
        if masked:
            row = i * tq + (lax.broadcasted_iota(jnp.int32, z.shape, 0) & (tq - 1))
            col = start + lax.broadcasted_iota(jnp.int32, z.shape, 1)
            strict = col < row
            log_keep = jnp.where(strict, log_keep, 0.0)
        hi = log_keep.astype(BF16)
        lo = (log_keep - hi.astype(F32)).astype(BF16)
        within = (jnp.dot(hi, suffix, preferred_element_type=F32)
                  + jnp.dot(lo, suffix, preferred_element_type=F32))
        later = c_ref[...] + within
        a = jnp.exp(z - softplus + later)
        if masked:
            a = jnp.where(strict, a, 0.0)
        acc_ref[...] += jnp.dot(a.astype(BF16), vt, preferred_element_type=F32)
        c_ref[...] += jnp.sum(log_keep, axis=1, keepdims=True)

    def masked_body(r, carry):
        step(n_tot - 1 - r, True)
        return carry

    def full_body(r, carry):
        step(n_full - 1 - r, False)
        return carry

    lax.fori_loop(0, n_tot - n_full, masked_body, 0)
    lax.fori_loop(0, n_full, full_body, 0)

    lane = lax.broadcasted_iota(jnp.int32, (tq, LANES), 1)
    low = lane < HEAD_DIM
    o = jnp.where(low, acc_ref[0:tq, :], acc_ref[tq:2 * tq, :])
    sq = o * o
    ms_low = jnp.sum(jnp.where(low, sq, 0.0), axis=1, keepdims=True) * (1.0 / HEAD_DIM)
    ms_high = jnp.sum(jnp.where(low, 0.0, sq), axis=1, keepdims=True) * (1.0 / HEAD_DIM)
    o = o * lax.rsqrt(jnp.where(low, ms_low, ms_high) + RMS_EPS) * g_ref[...]
    o_ref[0] = o.astype(BF16)


def _sb_attn(qs, ks, vs, g2):
    bsz, npair, seq, _ = qs.shape
    tq, tk = ATT_TQ, SB_TK
    kv_spec = pl.BlockSpec((1, 1, seq, LANES), lambda b, h, i: (b, h, 0, 0))
    return pl.pallas_call(
        functools.partial(_sb_attn_kernel, tq=tq, tk=tk),
        grid=(bsz, npair, seq // tq),
        in_specs=[pl.BlockSpec((1, LANES), lambda b, h, i: (0, 0)),
                  pl.BlockSpec((1, 1, tq, LANES), lambda b, h, i: (b, h, i, 0)),
                  kv_spec, kv_spec],
        out_specs=pl.BlockSpec((1, tq, LANES), lambda b, h, i: (b, i, h)),
        out_shape=jax.ShapeDtypeStruct((bsz, seq, npair * LANES), BF16),
        scratch_shapes=[pltpu.VMEM((2 * tq, LANES), BF16),
                        pltpu.VMEM((2 * tq, 1), F32),
                        pltpu.VMEM((2 * tq, LANES), F32)],
        compiler_params=pltpu.CompilerParams(
            dimension_semantics=("arbitrary", "arbitrary", "arbitrary"), vmem_limit_bytes=VMEM_LIMIT),
        name="sb_attn",
    )(g2, qs, ks, vs)


def _out_proj_kernel(d_ref, s_ref, x_ref, w_ref, g_ref, b_ref, o_ref):
    y = (jnp.dot(d_ref[0], w_ref[0:DIFF_WIDTH, :], preferred_element_type=F32)
         + jnp.dot(s_ref[0], w_ref[DIFF_WIDTH:D_MODEL, :], preferred_element_type=F32)
         + DEEPNORM_ALPHA * x_ref[0])
    o_ref[0] = _layer_norm(y, g_ref[...], b_ref[...])


def _out_proj(d_out, s_out, x, w_bf, g, b):
    bsz, seq, d = x.shape
    tm = PROJ_TM
    vec = pl.BlockSpec((1, d), lambda bb, i: (0, 0))
    return pl.pallas_call(
        _out_proj_kernel,
        grid=(bsz, seq // tm),
        in_specs=[pl.BlockSpec((1, tm, DIFF_WIDTH), lambda bb, i: (bb, i, 0)),
                  pl.BlockSpec((1, tm, SB_WIDTH), lambda bb, i: (bb, i, 0)),
                  pl.BlockSpec((1, tm, d), lambda bb, i: (bb, i, 0)),
                  pl.BlockSpec((d, d), lambda bb, i: (0, 0)),
                  vec, vec],
        out_specs=pl.BlockSpec((1, tm, d), lambda bb, i: (bb, i, 0)),
        out_shape=jax.ShapeDtypeStruct((bsz, seq, d), F32),
        compiler_params=pltpu.CompilerParams(
            dimension_semantics=("arbitrary", "arbitrary"), vmem_limit_bytes=VMEM_LIMIT),
        name="out_proj",
    )(d_out, s_out, x, w_bf, g, b)


def _mlp_kernel(x_ref, wup_ref, cw_ref, cb_ref, wdn_ref, g_ref, b_ref, o_ref,
                halo_ref, tail_ref, acc_ref, *, tm, tf, n_chunks):
    j = pl.program_id(1)

    @pl.when(j == 0)
    def _():
        tail_ref[...] = jnp.zeros(tail_ref.shape, F32)

    x = x_ref[0]
    xb = x.astype(BF16)
    acc_ref[...] = jnp.zeros(acc_ref.shape, F32)
    for c in range(n_chunks):
        halves = []
        for half in range(2):
            col0 = half * D_FF + c * tf
            slot = 2 * c + half
            up = jnp.dot(xb, wup_ref[:, col0:col0 + tf], preferred_element_type=F32)
            halo_ref[half, 0:SUBLANES, :] = tail_ref[slot]
            halo_ref[half, SUBLANES:SUBLANES + tm, :] = up
            tail_ref[slot] = up[tm - SUBLANES:tm, :]
            w = cw_ref[:, col0:col0 + tf]
            u = (cb_ref[:, col0:col0 + tf]
                 + halo_ref[half, SUBLANES - 2:SUBLANES - 2 + tm, :] * w[0:1, :]
                 + halo_ref[half, SUBLANES - 1:SUBLANES - 1 + tm, :] * w[1:2, :]
                 + up * w[2:3, :])
            halves.append(u)
        gate, val = halves
        act = gate * jax.nn.sigmoid(gate) * val
        acc_ref[...] += jnp.dot(act.astype(BF16), wdn_ref[c * tf:(c + 1) * tf, :], preferred_element_type=F32)
    y = DEEPNORM_ALPHA * x + acc_ref[...]
    o_ref[0] = _layer_norm(y, g_ref[...], b_ref[...])


def _mlp(x, wup_bf, conv_w, conv_b, wdn_bf, g, b):
    bsz, seq, d = x.shape
    tm, tf, n_chunks = MLP_TM, MLP_TF, MLP_CHUNKS
    vec = pl.BlockSpec((1, d), lambda bb, i: (0, 0))
    return pl.pallas_call(
        functools.partial(_mlp_kernel, tm=tm, tf=tf, n_chunks=n_chunks),
        grid=(bsz, seq // tm),
        in_specs=[pl.BlockSpec((1, tm, d), lambda bb, i: (bb, i, 0)),
                  pl.BlockSpec((d, 2 * D_FF), lambda bb, i: (0, 0)),
                  pl.BlockSpec((CONV_WIDTH, 2 * D_FF), lambda bb, i: (0, 0)),
                  pl.BlockSpec((1, 2 * D_FF), lambda bb, i: (0, 0)),
                  pl.BlockSpec((D_FF, d), lambda bb, i: (0, 0)),
                  vec, vec],
        out_specs=pl.BlockSpec((1, tm, d), lambda bb, i: (bb, i, 0)),
        out_shape=jax.ShapeDtypeStruct((bsz, seq, d), F32),
        scratch_shapes=[pltpu.VMEM((2, tm + SUBLANES, tf), F32),
                        pltpu.VMEM((2 * n_chunks, SUBLANES, tf), F32),
                        pltpu.VMEM((tm, d), F32)],
        compiler_params=pltpu.CompilerParams(
            dimension_semantics=("arbitrary", "arbitrary"), vmem_limit_bytes=VMEM_LIMIT),
        name="mlp",
    )(x, wup_bf, conv_w, conv_b, wdn_bf, g, b)


def _rope_tables(seq):
    inv = 1.0 / (ROPE_THETA ** (jnp.arange(0, HEAD_DIM, 2, dtype=F32) / HEAD_DIM))
    ang = jnp.arange(seq, dtype=F32)[:, None] * inv[None, :]
    cos, sin = jnp.cos(ang), jnp.sin(ang)
    cos_t = jnp.tile(cos, (1, LANES // (HEAD_DIM // 2)))
    sin_t = jnp.tile(jnp.concatenate([-sin, sin], axis=1), (1, LANES // HEAD_DIM))
    return cos_t, sin_t


def kernel(x, w_in, w_out, lam_q1, lam_k1, lam_q2, lam_k2, diff_norm_g, sb_norm_g,
           ln1_g, ln1_b, w_up, conv_w, conv_b, w_down, ln2_g, ln2_b):
    bsz, seq, d = x.shape
    cos_t, sin_t = _rope_tables(seq)
    x = x.astype(F32)
    for l in range(DEPTH):
        lam_init = 0.8 - 0.6 * math.exp(-0.3 * l)
        row = lambda v: v[l].astype(F32).reshape(1, -1)
        qd, kd, vd, qs, ks, vs = _in_proj(x, w_in[l].astype(BF16), cos_t, sin_t)
        d_out = _diff_attn(qd, kd, vd, row(lam_q1), row(lam_k1), row(lam_q2), row(lam_k2),
                           row(diff_norm_g), lam_init)
        g2 = jnp.tile(row(sb_norm_g), (1, LANES // HEAD_DIM))
        s_out = _sb_attn(qs, ks, vs, g2)
        x = _out_proj(d_out, s_out, x, w_out[l].astype(BF16), row(ln1_g), row(ln1_b))
        x = _mlp(x, w_up[l].astype(BF16), conv_w[l].astype(F32), row(conv_b),
                 w_down[l].astype(BF16), row(ln2_g), row(ln2_b))
    return x
```

```python
import functools
import math

import jax
import jax.numpy as jnp
from jax import lax
from jax.experimental import pallas as pl
from jax.experimental.pallas import tpu as pltpu

D_MODEL = 1024
HEAD_DIM = 64
DIFF_WIDTH = 512
SB_WIDTH = 512
D_FF = 2816
CONV_WIDTH = 3
ROPE_THETA = 10000.0
LN_EPS = 1e-5
RMS_EPS = 1e-6
DEPTH = 2
DEEPNORM_ALPHA = (2 * DEPTH) ** 0.25
QK_SCALE = HEAD_DIM ** -0.5
F32_EXP_UNDERFLOW = -104.0

LANES = 128
SUBLANES = 8
VMEM_LIMIT = 56 * 1024 * 1024

PROJ_TM = 512
PROJ_GROUP = 512
DIFF_TQ = 512
DIFF_TK = 512
SB_TQ = 256
SB_TK = 256
CHAIN = 256
MLP_TM = 512
MLP_TF = 256
MLP_CHUNKS = D_FF // MLP_TF

F32 = jnp.float32
BF16 = jnp.bfloat16


def _layer_norm(y, g, b):
    mu = jnp.mean(y, axis=-1, keepdims=True)
    yc = y - mu
    var = jnp.mean(yc * yc, axis=-1, keepdims=True)
    return yc * lax.rsqrt(var + LN_EPS) * g + b


_ROW_MAJOR_GROUPS = (1, 4)


def _in_proj_kernel(x_ref, w_ref, cos_ref, sin_ref, qd_ref, kd_ref, vd_ref, qs_ref, ks_ref, vs_ref):
    xb = x_ref[0].astype(BF16)
    cos = cos_ref[...]
    sin = sin_ref[...]
    lane = lax.broadcasted_iota(jnp.int32, cos.shape, 1)
    first_half = (lane & (HEAD_DIM - 1)) < (HEAD_DIM // 2)
    outs = (qd_ref, kd_ref, vd_ref, qs_ref, ks_ref, vs_ref)
    for grp, o_ref in enumerate(outs):
        acc = jnp.dot(xb, w_ref[:, grp * PROJ_GROUP:(grp + 1) * PROJ_GROUP], preferred_element_type=F32)
        for hh in range(PROJ_GROUP // LANES):
            a = acc[:, hh * LANES:(hh + 1) * LANES]
            if grp < 2:
                partner = jnp.where(first_half,
                                    pltpu.roll(a, LANES - HEAD_DIM // 2, 1),
                                    pltpu.roll(a, HEAD_DIM // 2, 1))
                a = a * cos + partner * sin
            if grp in (0, 3):
                a = a * QK_SCALE
            if grp in _ROW_MAJOR_GROUPS:
                o_ref[0, hh] = a.astype(BF16)
            else:
                o_ref[0, hh] = a.T.astype(BF16)


def _in_proj(x, w_bf, cos_t, sin_t):
    bsz, seq, d = x.shape
    tm = PROJ_TM
    row_major = (jax.ShapeDtypeStruct((bsz, 4, seq, LANES), BF16),
                 pl.BlockSpec((1, 4, tm, LANES), lambda b, i: (b, 0, i, 0)))
    transposed = (jax.ShapeDtypeStruct((bsz, 4, LANES, seq), BF16),
                  pl.BlockSpec((1, 4, LANES, tm), lambda b, i: (b, 0, 0, i)))
    kinds = [row_major if grp in _ROW_MAJOR_GROUPS else transposed for grp in range(6)]
    return pl.pallas_call(
        _in_proj_kernel,
        grid=(bsz, seq // tm),
        in_specs=[
            pl.BlockSpec((1, tm, d), lambda b, i: (b, i, 0)),
            pl.BlockSpec((d, 3 * d), lambda b, i: (0, 0)),
            pl.BlockSpec((tm, LANES), lambda b, i: (i, 0)),
            pl.BlockSpec((tm, LANES), lambda b, i: (i, 0)),
        ],
        out_specs=[k[1] for k in kinds],
        out_shape=[k[0] for k in kinds],
        compiler_params=pltpu.CompilerParams(
            dimension_semantics=("arbitrary", "arbitrary"), vmem_limit_bytes=VMEM_LIMIT),
        name="in_proj",
    )(x, w_bf, cos_t, sin_t)


def _stack_halves(q_t, qs_ref, tq):
    row = lax.broadcasted_iota(jnp.int32, q_t.shape, 0)
    zero = jnp.zeros_like(q_t)
    qs_ref[:, 0:tq] = jnp.where(row < HEAD_DIM, q_t, zero)
    qs_ref[:, tq:2 * tq] = jnp.where(row < HEAD_DIM, zero, q_t)


def _tile_counts(i, tq, tk):
    n_full = (i * tq) // tk
    n_tot = (i * tq + tq + tk - 1) // tk
    return n_full, n_tot


def _key_query_ids(shape, start, i, tq, lane0):
    key = start + lax.broadcasted_iota(jnp.int32, shape, 0)
    qry = i * tq + ((lane0 + lax.broadcasted_iota(jnp.int32, shape, 1)) & (tq - 1))
    return key, qry


def _attn_specs(seq, tq):
    return [pl.BlockSpec((1, 1, LANES, tq), lambda b, h, i: (b, h, 0, i)),
            pl.BlockSpec((1, 1, seq, LANES), lambda b, h, i: (b, h, 0, 0)),
            pl.BlockSpec((1, 1, LANES, seq), lambda b, h, i: (b, h, 0, 0))]


def _diff_attn_kernel(lq1_ref, lk1_ref, lq2_ref, lk2_ref, g_ref, q_ref, k_ref, v_ref, o_ref,
                      qs_ref, m_ref, l_ref, acc_ref, *, tq, tk, lam_init):
    i = pl.program_id(2)
    _stack_halves(q_ref[0, 0], qs_ref, tq)
    m_ref[...] = jnp.full(m_ref.shape, -jnp.inf, F32)
    l_ref[...] = jnp.zeros(l_ref.shape, F32)
    acc_ref[...] = jnp.zeros(acc_ref.shape, F32)
    n_full, n_tot = _tile_counts(i, tq, tk)

    def step(kj, masked):
        start = pl.multiple_of(kj * tk, tk)
        kt = k_ref[0, 0, pl.ds(start, tk), :]
        vt = v_ref[0, 0, :, pl.ds(start, tk)]
        slices = [slice(lane0, lane0 + CHAIN) for lane0 in range(0, 2 * tq, CHAIN)]
        m_old = m_ref[...]
        l_old = l_ref[...]
        ss = [jnp.dot(kt, qs_ref[:, sl], preferred_element_type=F32) for sl in slices]
        if masked:
            masked_ss = []
            for sl, s in zip(slices, ss):
                key, qry = _key_query_ids(s.shape, start, i, tq, sl.start)
                masked_ss.append(jnp.where(key <= qry, s, -jnp.inf))
            ss = masked_ss
        m_news = [jnp.maximum(m_old[:, sl], jnp.max(s, axis=0, keepdims=True)) for sl, s in zip(slices, ss)]
        ps = [jnp.exp(s - m_new) for s, m_new in zip(ss, m_news)]
        alphas = [jnp.exp(m_old[:, sl] - m_new) for sl, m_new in zip(slices, m_news)]
        pvs = [jnp.dot(vt, p.astype(BF16), preferred_element_type=F32) for p in ps]
        for sl, p, alpha, pv, m_new in zip(slices, ps, alphas, pvs, m_news):
            l_ref[:, sl] = alpha * l_old[:, sl] + jnp.sum(p, axis=0, keepdims=True)
            acc_ref[:, sl] = alpha * acc_ref[:, sl] + pv
            m_ref[:, sl] = m_new

    def full_body(kj, carry):
        step(kj, False)
        return carry

    def masked_body(kj, carry):
        step(kj, True)
        return carry

    lax.fori_loop(0, n_full, full_body, 0)
    lax.fori_loop(n_full, n_tot, masked_body, 0)

    lam = (jnp.exp(jnp.sum(lq1_ref[...] * lk1_ref[...], axis=1, keepdims=True))
           - jnp.exp(jnp.sum(lq2_ref[...] * lk2_ref[...], axis=1, keepdims=True)) + lam_init)
    o1 = acc_ref[:, 0:tq] / l_ref[:, 0:tq]
    o2 = acc_ref[:, tq:2 * tq] / l_ref[:, tq:2 * tq]
    o = o1 - lam * o2
    o = o * lax.rsqrt(jnp.mean(o * o, axis=0, keepdims=True) + RMS_EPS) * g_ref[...]
    o_ref[0] = (o * (1.0 - lam_init)).T.astype(BF16)


def _diff_attn(qd_t, kd, vd_t, lq1, lk1, lq2, lk2, g_col, lam_init):
    bsz, nh, seq, _ = kd.shape
    tq, tk = DIFF_TQ, DIFF_TK
    vec64 = pl.BlockSpec((1, HEAD_DIM), lambda b, h, i: (0, 0))
    return pl.pallas_call(
        functools.partial(_diff_attn_kernel, tq=tq, tk=tk, lam_init=lam_init),
        grid=(bsz, nh, seq // tq),
        in_specs=[vec64, vec64, vec64, vec64,
                  pl.BlockSpec((LANES, 1), lambda b, h, i: (0, 0))] + _attn_specs(seq, tq),
        out_specs=pl.BlockSpec((1, tq, LANES), lambda b, h, i: (b, i, h)),
        out_shape=jax.ShapeDtypeStruct((bsz, seq, nh * LANES), BF16),
        scratch_shapes=[pltpu.VMEM((LANES, 2 * tq), BF16),
                        pltpu.VMEM((1, 2 * tq), F32),
                        pltpu.VMEM((1, 2 * tq), F32),
                        pltpu.VMEM((LANES, 2 * tq), F32)],
        compiler_params=pltpu.CompilerParams(
            dimension_semantics=("arbitrary", "arbitrary", "arbitrary"), vmem_limit_bytes=VMEM_LIMIT),
        name="diff_attn",
    )(lq1, lk1, lq2, lk2, g_col, qd_t, kd, vd_t)


def _sb_attn_kernel(g_ref, q_ref, k_ref, v_ref, o_ref, qs_ref, c_ref, acc_ref, *, tq, tk):
    i = pl.program_id(2)
    _stack_halves(q_ref[0, 0], qs_ref, tq)
    c_ref[...] = jnp.zeros(c_ref.shape, F32)
    acc_ref[...] = jnp.zeros(acc_ref.shape, F32)
    n_full, n_tot = _tile_counts(i, tq, tk)
    later_keys = (lax.broadcasted_iota(jnp.int32, (tk, tk), 1)
                  > lax.broadcasted_iota(jnp.int32, (tk, tk), 0)).astype(BF16)

    def step(kj, masked):
        start = pl.multiple_of(kj * tk, tk)
        kt = k_ref[0, 0, pl.ds(start, tk), :]
        vt = v_ref[0, 0, :, pl.ds(start, tk)]
        slices = [slice(lane0, lane0 + CHAIN) for lane0 in range(0, 2 * tq, CHAIN)]
        c_old = c_ref[...]
        zs = [jnp.dot(kt, qs_ref[:, sl], preferred_element_type=F32) for sl in slices]
        softpluses = [jnp.maximum(z, 0.0) + jnp.log(1.0 + jnp.exp(-jnp.abs(z))) for z in zs]
        log_keeps = [-sp for sp in softpluses]
        if masked:
            stricts = []
            for sl, z in zip(slices, zs):
                key, qry = _key_query_ids(z.shape, start, i, tq, sl.start)
                stricts.append(key < qry)
            log_keeps = [jnp.where(st, lk, 0.0) for st, lk in zip(stricts, log_keeps)]
        his = [lk.astype(BF16) for lk in log_keeps]
        los = [(lk - hi.astype(F32)).astype(BF16) for lk, hi in zip(log_keeps, his)]
        withins = [jnp.dot(later_keys, hi, preferred_element_type=F32)
                   + jnp.dot(later_keys, lo, preferred_element_type=F32) for hi, lo in zip(his, los)]
        weights = [jnp.exp(z - sp + (c_old[:, sl] + w)) for sl, z, sp, w in zip(slices, zs, softpluses, withins)]
        if masked:
            weights = [jnp.where(st, a, 0.0) for st, a in zip(stricts, weights)]
        avs = [jnp.dot(vt, a.astype(BF16), preferred_element_type=F32) for a in weights]
        for sl, av, lk in zip(slices, avs, log_keeps):
            acc_ref[:, sl] += av
            c_ref[:, sl] = c_old[:, sl] + jnp.sum(lk, axis=0, keepdims=True)

    def masked_body(r, carry):
        step(n_tot - 1 - r, True)
        return carry

    lax.fori_loop(0, n_tot - n_full, masked_body, 0)

    def keep_walking(carry):
        r, c_max = carry
        return jnp.logical_and(r < n_full, c_max > F32_EXP_UNDERFLOW)

    def full_body(carry):
        r, _ = carry
        step(n_full - 1 - r, False)
        return r + 1, jnp.max(c_ref[...])

    lax.while_loop(keep_walking, full_body, (jnp.int32(0), jnp.max(c_ref[...])))

    row = lax.broadcasted_iota(jnp.int32, (LANES, tq), 0)
    low = row < HEAD_DIM
    o = jnp.where(low, acc_ref[:, 0:tq], acc_ref[:, tq:2 * tq])
    sq = o * o
    ms_low = jnp.sum(jnp.where(low, sq, 0.0), axis=0, keepdims=True) * (1.0 / HEAD_DIM)
    ms_high = jnp.sum(jnp.where(low, 0.0, sq), axis=0, keepdims=True) * (1.0 / HEAD_DIM)
    o = o * lax.rsqrt(jnp.where(low, ms_low, ms_high) + RMS_EPS) * g_ref[...]
    o_ref[0] = o.T.astype(BF16)


def _sb_attn(qs_t, ks, vs_t, g_col):
    bsz, npair, seq, _ = ks.shape
    tq, tk = SB_TQ, SB_TK
    return pl.pallas_call(
        functools.partial(_sb_attn_kernel, tq=tq, tk=tk),
        grid=(bsz, npair, seq // tq),
        in_specs=[pl.BlockSpec((LANES, 1), lambda b, h, i: (0, 0))] + _attn_specs(seq, tq),
        out_specs=pl.BlockSpec((1, tq, LANES), lambda b, h, i: (b, i, h)),
        out_shape=jax.ShapeDtypeStruct((bsz, seq, npair * LANES), BF16),
        scratch_shapes=[pltpu.VMEM((LANES, 2 * tq), BF16),
                        pltpu.VMEM((1, 2 * tq), F32),
                        pltpu.VMEM((LANES, 2 * tq), F32)],
        compiler_params=pltpu.CompilerParams(
            dimension_semantics=("arbitrary", "arbitrary", "arbitrary"), vmem_limit_bytes=VMEM_LIMIT),
        name="sb_attn",
    )(g_col, qs_t, ks, vs_t)


def _out_proj_kernel(d_ref, s_ref, x_ref, w_ref, g_ref, b_ref, o_ref):
    y = (jnp.dot(d_ref[0], w_ref[0:DIFF_WIDTH, :], preferred_element_type=F32)
         + jnp.dot(s_ref[0], w_ref[DIFF_WIDTH:D_MODEL, :], preferred_element_type=F32)
         + DEEPNORM_ALPHA * x_ref[0])
    o_ref[0] = _layer_norm(y, g_ref[...], b_ref[...])


def _out_proj(d_out, s_out, x, w_bf, g, b):
    bsz, seq, d = x.shape
    tm = PROJ_TM
    vec = pl.BlockSpec((1, d), lambda bb, i: (0, 0))
    return pl.pallas_call(
        _out_proj_kernel,
        grid=(bsz, seq // tm),
        in_specs=[pl.BlockSpec((1, tm, DIFF_WIDTH), lambda bb, i: (bb, i, 0)),
                  pl.BlockSpec((1, tm, SB_WIDTH), lambda bb, i: (bb, i, 0)),
                  pl.BlockSpec((1, tm, d), lambda bb, i: (bb, i, 0)),
                  pl.BlockSpec((d, d), lambda bb, i: (0, 0)),
                  vec, vec],
        out_specs=pl.BlockSpec((1, tm, d), lambda bb, i: (bb, i, 0)),
        out_shape=jax.ShapeDtypeStruct((bsz, seq, d), F32),
        compiler_params=pltpu.CompilerParams(
            dimension_semantics=("arbitrary", "arbitrary"), vmem_limit_bytes=VMEM_LIMIT),
        name="out_proj",
    )(d_out, s_out, x, w_bf, g, b)


def _mlp_kernel(x_ref, wup_ref, cw_ref, cb_ref, wdn_ref, g_ref, b_ref, o_ref,
                halo_ref, tail_ref, acc_ref, *, tm, tf, n_chunks):
    j = pl.program_id(1)

    @pl.when(j == 0)
    def _():
        tail_ref[...] = jnp.zeros(tail_ref.shape, F32)

    x = x_ref[0]
    xb = x.astype(BF16)
    acc_ref[...] = jnp.zeros(acc_ref.shape, F32)
    for c in range(n_chunks):
        halves = []
        for half in range(2):
            col0 = half * D_FF + c * tf
            slot = 2 * c + half
            up = jnp.dot(xb, wup_ref[:, col0:col0 + tf], preferred_element_type=F32)
            halo_ref[half, 0:SUBLANES, :] = tail_ref[slot]
            halo_ref[half, SUBLANES:SUBLANES + tm, :] = up
            tail_ref[slot] = up[tm - SUBLANES:tm, :]
            w = cw_ref[:, col0:col0 + tf]
            u = (cb_ref[:, col0:col0 + tf]
                 + halo_ref[half, SUBLANES - 2:SUBLANES - 2 + tm, :] * w[0:1, :]
                 + halo_ref[half, SUBLANES - 1:SUBLANES - 1 + tm, :] * w[1:2, :]
                 + up * w[2:3, :])
            halves.append(u)
        gate, val = halves
        act = gate * jax.nn.sigmoid(gate) * val
        acc_ref[...] += jnp.dot(act.astype(BF16), wdn_ref[c * tf:(c + 1) * tf, :], preferred_element_type=F32)
    y = DEEPNORM_ALPHA * x + acc_ref[...]
    o_ref[0] = _layer_norm(y, g_ref[...], b_ref[...])


def _mlp(x, wup_bf, conv_w, conv_b, wdn_bf, g, b):
    bsz, seq, d = x.shape
    tm, tf, n_chunks = MLP_TM, MLP_TF, MLP_CHUNKS
    vec = pl.BlockSpec((1, d), lambda bb, i: (0, 0))
    return pl.pallas_call(
        functools.partial(_mlp_kernel, tm=tm, tf=tf, n_chunks=n_chunks),
        grid=(bsz, seq // tm),
        in_specs=[pl.BlockSpec((1, tm, d), lambda bb, i: (bb, i, 0)),
                  pl.BlockSpec((d, 2 * D_FF), lambda bb, i: (0, 0)),
                  pl.BlockSpec((CONV_WIDTH, 2 * D_FF), lambda bb, i: (0, 0)),
                  pl.BlockSpec((1, 2 * D_FF), lambda bb, i: (0, 0)),
                  pl.BlockSpec((D_FF, d), lambda bb, i: (0, 0)),
                  vec, vec],
        out_specs=pl.BlockSpec((1, tm, d), lambda bb, i: (bb, i, 0)),
        out_shape=jax.ShapeDtypeStruct((bsz, seq, d), F32),
        scratch_shapes=[pltpu.VMEM((2, tm + SUBLANES, tf), F32),
                        pltpu.VMEM((2 * n_chunks, SUBLANES, tf), F32),
                        pltpu.VMEM((tm, d), F32)],
        compiler_params=pltpu.CompilerParams(
            dimension_semantics=("arbitrary", "arbitrary"), vmem_limit_bytes=VMEM_LIMIT),
        name="mlp",
    )(x, wup_bf, conv_w, conv_b, wdn_bf, g, b)


def _rope_tables(seq):
    inv = 1.0 / (ROPE_THETA ** (jnp.arange(0, HEAD_DIM, 2, dtype=F32) / HEAD_DIM))
    ang = jnp.arange(seq, dtype=F32)[:, None] * inv[None, :]
    cos, sin = jnp.cos(ang), jnp.sin(ang)
    cos_t = jnp.tile(cos, (1, LANES // (HEAD_DIM // 2)))
    sin_t = jnp.tile(jnp.concatenate([-sin, sin], axis=1), (1, LANES // HEAD_DIM))
    return cos_t, sin_t


def kernel(x, w_in, w_out, lam_q1, lam_k1, lam_q2, lam_k2, diff_norm_g, sb_norm_g,
           ln1_g, ln1_b, w_up, conv_w, conv_b, w_down, ln2_g, ln2_b):
    bsz, seq, d = x.shape
    cos_t, sin_t = _rope_tables(seq)
    x = x.astype(F32)
    for l in range(DEPTH):
        lam_init = 0.8 - 0.6 * math.exp(-0.3 * l)
        row = lambda v: v[l].astype(F32).reshape(1, -1)
        col = lambda v: v.astype(F32).reshape(-1, 1)
        qd_t, kd, vd_t, qs_t, ks, vs_t = _in_proj(x, w_in[l].astype(BF16), cos_t, sin_t)
        d_out = _diff_attn(qd_t, kd, vd_t, row(lam_q1), row(lam_k1), row(lam_q2), row(lam_k2),
                           col(diff_norm_g[l]), lam_init)
        s_out = _sb_attn(qs_t, ks, vs_t, col(jnp.tile(sb_norm_g[l], LANES // HEAD_DIM)))
        x = _out_proj(d_out, s_out, x, w_out[l].astype(BF16), row(ln1_g), row(ln1_b))
        x = _mlp(x, w_up[l].astype(BF16), conv_w[l].astype(F32), row(conv_b),
                 w_down[l].astype(BF16), row(ln2_g), row(ln2_b))
    return x
```
